```python
import math
import jax, jax.numpy as jnp
from jax import lax
import numpy as np

D_MODEL = 4096
BATCH = 2
SEQ = 4096
DEPTH = 2

HEAD_DIM = 128
MIX_WIDTH = D_MODEL
DIFF_WIDTH = MIX_WIDTH // 2
DIL_WIDTH = MIX_WIDTH - DIFF_WIDTH
N_DIFF_HEADS = DIFF_WIDTH // (2 * HEAD_DIM)
N_DIL_HEADS = DIL_WIDTH // HEAD_DIM
IN_WIDTH = 3 * DIFF_WIDTH + 3 * DIL_WIDTH
IN_SPLITS = (DIFF_WIDTH, 2 * DIFF_WIDTH, 3 * DIFF_WIDTH,
             3 * DIFF_WIDTH + DIL_WIDTH, 3 * DIFF_WIDTH + 2 * DIL_WIDTH)
DIL_PATTERNS = ((128, 1), (512, 4), (2048, 16))
D_FF = 256 * ((8 * D_MODEL // 3 + 255) // 256)
CONV_WIDTH = 3
NUM_BUCKETS = 32
MAX_DISTANCE = 2048
Q_BLOCK = 128
LN_EPS = 1e-5
NEG_INF = -1e30
DEEPNORM_ALPHA = (2 * DEPTH) ** 0.25
DEEPNORM_BETA = (8 * DEPTH) ** -0.25

kernel_name = "hymba_style_diff_dilated_convglu_deepnorm_adaln"


def ln_plain(x):
    xf = x.astype(jnp.float32)
    mu = jnp.mean(xf, axis=-1, keepdims=True)
    var = jnp.mean(jnp.square(xf - mu), axis=-1, keepdims=True)
    return ((xf - mu) * lax.rsqrt(var + LN_EPS)).astype(x.dtype)


def layer_norm(x, g, b):
    return ln_plain(x) * g + b


def rms_norm(x, g):
    xf = x.astype(jnp.float32)
    y = xf * lax.rsqrt(jnp.mean(jnp.square(xf), axis=-1, keepdims=True) + LN_EPS)
    return y.astype(x.dtype) * g


def t5_bucket(dist):
    n = jnp.maximum(dist, 0)
    max_exact = NUM_BUCKETS // 2
    nf = jnp.maximum(n, max_exact).astype(jnp.float32)
    large = max_exact + (jnp.log(nf / max_exact) / math.log(MAX_DISTANCE / max_exact)
                         * (NUM_BUCKETS - max_exact)).astype(jnp.int32)
    large = jnp.minimum(large, NUM_BUCKETS - 1)
    return jnp.where(n < max_exact, n, large)


def diff_attention(q, k, v, lam, bias_tab):
    B, S, H, _, E = q.shape
    scale = E ** -0.5
    kpos = jnp.arange(S)

    def block(n):
        start = n * Q_BLOCK
        qb = lax.dynamic_slice_in_dim(q, start, Q_BLOCK, axis=1)
        logits = jnp.einsum("bqhme,bkhme->bhmqk", qb, k).astype(jnp.float32) * scale
        dist = (start + jnp.arange(Q_BLOCK))[:, None] - kpos[None, :]
        bias = jnp.transpose(bias_tab[t5_bucket(dist)], (2, 0, 1)).astype(jnp.float32)
        logits = jnp.where(dist >= 0, logits + bias[None, :, None], NEG_INF)
        p = jax.nn.softmax(logits, axis=-1)
        attn = p[:, :, 0] - lam * p[:, :, 1]
        return jnp.einsum("bhqk,bkhe->bqhe", attn.astype(v.dtype), v)

    out = lax.map(block, jnp.arange(S // Q_BLOCK))
    return jnp.transpose(out, (1, 0, 2, 3, 4)).reshape(B, S, H, v.shape[-1])


def dilated_branch(q, k, v, window, dil, bias_tab):
    B, S, H, E = q.shape
    span = window // dil
    blk = span
    chunk = dil * blk
    Sp = -(-S // chunk) * chunk
    L = Sp // dil
    nb = L // blk

    def to_classes(t):
        t = jnp.pad(t, ((0, 0), (0, Sp - S), (0, 0), (0, 0)))
        t = jnp.swapaxes(t.reshape(B, L, dil, H, E), 1, 2)
        return t.reshape(B, dil, nb, blk, H, E)

    def with_prev(t):
        prev = jnp.concatenate([jnp.zeros_like(t[:, :, :1]), t[:, :, :-1]], axis=2)
        return jnp.concatenate([prev, t], axis=3)

    qc = to_classes(q)
    kw = with_prev(to_classes(k))
    vw = with_prev(to_classes(v))
    logits = jnp.einsum("bdnqhe,bdnkhe->bdnhqk", qc, kw).astype(jnp.float32) * (E ** -0.5)
    i = jnp.arange(blk)[:, None]
    j = jnp.arange(2 * blk)[None, :]
    cdist = i + blk - j
    bias = jnp.transpose(bias_tab[t5_bucket(cdist * dil)], (2, 0, 1)).astype(jnp.float32)
    band = (cdist >= 0) & (cdist <= span)
    has_prev = (jnp.arange(nb)[:, None, None] > 0) | (j[None] >= blk)
    valid = band[None] & has_prev
    logits = jnp.where(valid[None, None, :, None], logits + bias[None, None, None], NEG_INF)
    m = jnp.max(logits, axis=-1, keepdims=True)
    p = jnp.exp(logits - m)
    s = jnp.sum(p, axis=-1)
    o = jnp.einsum("bdnhqk,bdnkhe->bdnqhe", p.astype(v.dtype), vw).astype(jnp.float32)
    o = o / jnp.swapaxes(s, 3, 4)[..., None]
    lse = jnp.swapaxes(m[..., 0] + jnp.log(s), 3, 4)

    def from_classes(t):
        rest = t.shape[4:]
        t = jnp.swapaxes(t.reshape((B, dil, L) + rest), 1, 2)
        return t.reshape((B, Sp) + rest)[:, :S]

    return from_classes(o), from_classes(lse)


def dilated_attention(q, k, v, bias_tab):
    outs, lses = [], []
    for window, dil in DIL_PATTERNS:
        o, lse = dilated_branch(q, k, v, window, dil, bias_tab)
        outs.append(o)
        lses.append(lse)
    w = jax.nn.softmax(jnp.stack(lses), axis=0)
    return jnp.einsum("pbsh,pbshe->bshe", w, jnp.stack(outs)).astype(q.dtype)


def mixer_sublayer(u, w_in, lq1, lk1, lq2, lk2, lam_init, g_diff, g_dil, w_o, rel_bias):
    B, S, _ = u.shape
    qa, ka, va, qb, kb, vb = jnp.split(u @ w_in, IN_SPLITS, axis=-1)
    f32 = jnp.float32
    lam = (jnp.exp(jnp.sum(lq1.astype(f32) * lk1.astype(f32)))
           - jnp.exp(jnp.sum(lq2.astype(f32) * lk2.astype(f32))) + lam_init)
    ya = diff_attention(qa.reshape(B, S, N_DIFF_HEADS, 2, HEAD_DIM),
                        ka.reshape(B, S, N_DIFF_HEADS, 2, HEAD_DIM),
                        va.reshape(B, S, N_DIFF_HEADS, 2 * HEAD_DIM),
                        lam, rel_bias[:, :N_DIFF_HEADS])
    ya = rms_norm(ya, g_diff) * (1.0 - lam_init)
    yb = dilated_attention(qb.reshape(B, S, N_DIL_HEADS, HEAD_DIM),
                           kb.reshape(B, S, N_DIL_HEADS, HEAD_DIM),
                           vb.reshape(B, S, N_DIL_HEADS, HEAD_DIM),
                           rel_bias[:, N_DIFF_HEADS:])
    yb = rms_norm(yb, g_dil)
    y = jnp.concatenate([ya.reshape(B, S, DIFF_WIDTH), yb.reshape(B, S, DIL_WIDTH)], axis=-1)
    return y @ w_o


def conv_glu(u, w_up, conv_w, conv_b, w_down):
    g, v = jnp.split(u @ w_up, 2, axis=-1)
    g = lax.conv_general_dilated(g, conv_w[:, None, :], window_strides=(1,),
                                 padding=[(CONV_WIDTH - 1, 0)],
                                 dimension_numbers=("NWC", "WIO", "NWC"),
                                 feature_group_count=D_FF) + conv_b
    return (jax.nn.silu(g) * v) @ w_down


def setup_inputs(seed: int = 0) -> dict:
    key = jax.random.key(seed)
    ks = jax.random.split(key, 24)
    D, F = D_MODEL, D_FF

    def nrm(k, shape, s):
        return jax.random.normal(k, shape, jnp.float32) * s

    col_scale = jnp.concatenate([
        jnp.ones((2 * DIFF_WIDTH,), jnp.float32),
        jnp.full((DIFF_WIDTH,), DEEPNORM_BETA, jnp.float32),
        jnp.ones((2 * DIL_WIDTH,), jnp.float32),
        jnp.full((DIL_WIDTH,), DEEPNORM_BETA, jnp.float32)])
    return {
        "x": nrm(ks[0], (BATCH, SEQ, D), 1.0),
        "c": nrm(ks[1], (BATCH, D), 1.0),
        "w_ada": nrm(ks[2], (DEPTH, D, 6 * D), D ** -0.5),
        "b_ada": nrm(ks[3], (DEPTH, 6 * D), 0.01),
        "w_in": nrm(ks[4], (DEPTH, D, IN_WIDTH), D ** -0.5) * col_scale,
        "lambda_q1": nrm(ks[5], (DEPTH, HEAD_DIM), 0.1),
        "lambda_k1": nrm(ks[6], (DEPTH, HEAD_DIM), 0.1),
        "lambda_q2": nrm(ks[7], (DEPTH, HEAD_DIM), 0.1),
        "lambda_k2": nrm(ks[8], (DEPTH, HEAD_DIM), 0.1),
        "g_diff": 1.0 + nrm(ks[9], (DEPTH, 2 * HEAD_DIM), 0.02),
        "g_dil": 1.0 + nrm(ks[10], (DEPTH, HEAD_DIM), 0.02),
        "w_o": nrm(ks[11], (DEPTH, MIX_WIDTH, D), MIX_WIDTH ** -0.5 * DEEPNORM_BETA),
        "ln1_g": 1.0 + nrm(ks[12], (DEPTH, D), 0.02),
        "ln1_b": nrm(ks[13], (DEPTH, D), 0.02),
        "w_up": nrm(ks[14], (DEPTH, D, 2 * F), D ** -0.5),
        "conv_w": nrm(ks[15], (DEPTH, CONV_WIDTH, F), CONV_WIDTH ** -0.5),
        "conv_b": nrm(ks[16], (DEPTH, F), 0.02),
        "w_down": nrm(ks[17], (DEPTH, F, D), F ** -0.5 * DEEPNORM_BETA),
        "ln2_g": 1.0 + nrm(ks[18], (DEPTH, D), 0.02),
        "ln2_b": nrm(ks[19], (DEPTH, D), 0.02),
        "rel_bias": nrm(ks[20], (NUM_BUCKETS, N_DIFF_HEADS + N_DIL_HEADS), 0.5),
    }


def reference(x, c, w_ada, b_ada, w_in, lambda_q1, lambda_k1, lambda_q2, lambda_k2,
              g_diff, g_dil, w_o, ln1_g, ln1_b, w_up, conv_w, conv_b, w_down,
              ln2_g, ln2_b, rel_bias):
    for l in range(DEPTH):
        lam_init = 0.8 - 0.6 * math.exp(-0.3 * l)
        mod = jax.nn.silu(c) @ w_ada[l] + b_ada[l]
        sh_a, sc_a, gt_a, sh_f, sc_f, gt_f = [m[:, None, :] for m in jnp.split(mod, 6, axis=-1)]
        u = ln_plain(x) * (1.0 + sc_a) + sh_a
        y = mixer_sublayer(u, w_in[l], lambda_q1[l], lambda_k1[l], lambda_q2[l], lambda_k2[l],
                           lam_init, g_diff[l], g_dil[l], w_o[l], rel_bias)
        x = layer_norm(DEEPNORM_ALPHA * x + gt_a * y, ln1_g[l], ln1_b[l])
        u = ln_plain(x) * (1.0 + sc_f) + sh_f
        y = conv_glu(u, w_up[l], conv_w[l], conv_b[l], w_down[l])
        x = layer_norm(DEEPNORM_ALPHA * x + gt_f * y, ln2_g[l], ln2_b[l])
    return x
```

```python
import functools
import math

import numpy as np
import jax
import jax.numpy as jnp
from jax import lax
from jax.experimental import pallas as pl
from jax.experimental.pallas import tpu as pltpu

F32 = jnp.float32
BF16 = jnp.bfloat16

HEAD_DIM = 128
NUM_BUCKETS = 32
MAX_DISTANCE = 2048
DIL_PATTERNS = ((128, 1), (512, 4), (2048, 16))
CONV_WIDTH = 3
LN_EPS = 1e-5
NEG_INF = -1e30
N_MOD = 6

V7X_SUBLANES = 8
V7X_MXU_DIM = 256
V7X_VMEM_LIMIT_CAP = 60 * 1024 * 1024

DIFF_TILE = 256
DIL_TILE = 128
GLU_TILE = 256


def _bucket_starts():
    n = np.arange(MAX_DISTANCE + 1)
    max_exact = NUM_BUCKETS // 2
    nf = np.maximum(n, max_exact).astype(np.float64)
    large = max_exact + (np.log(nf / max_exact) / math.log(MAX_DISTANCE / max_exact)
                         * (NUM_BUCKETS - max_exact)).astype(np.int64)
    bucket = np.where(n < max_exact, n, np.minimum(large, NUM_BUCKETS - 1))
    assert np.all(np.diff(bucket) >= 0) and bucket[-1] == NUM_BUCKETS - 1
    return tuple(int(np.argmax(bucket >= b)) for b in range(NUM_BUCKETS))


BUCKET_STARTS = _bucket_starts()


def _dil_count_np(dist):
    cnt = np.zeros(dist.shape, np.int64)
    for window, dil in DIL_PATTERNS:
        cnt += (dist >= 0) & (dist <= window) & (dist % dil == 0)
    return cnt


def _params(semantics, vmem_bytes):
    return pltpu.CompilerParams(
        dimension_semantics=semantics,
        vmem_limit_bytes=int(min(vmem_bytes, V7X_VMEM_LIMIT_CAP)))


def _ln_plain(x):
    mu = jnp.mean(x, axis=-1, keepdims=True)
    xc = x - mu
    var = jnp.mean(xc * xc, axis=-1, keepdims=True)
    return xc * lax.rsqrt(var + LN_EPS)


def _adaln_kernel(ct_ref, w_ref, b_ref, o_ref, s_ref, *, k_chunk):
    @pl.when((pl.program_id(0) == 0) & (pl.program_id(1) == 0))
    def _():
        ct = ct_ref[...]
        s_ref[...] = ct * jax.nn.sigmoid(ct)

    k_dim, tn = w_ref.shape[1], w_ref.shape[2]
    nb = ct_ref.shape[1]

    def body(t, accs):
        k0 = pl.multiple_of(t * k_chunk, k_chunk)
        wk = w_ref[0, pl.ds(k0, k_chunk), :]
        sk = s_ref[pl.ds(k0, k_chunk), :]
        out = []
        for b in range(nb):
            prod = wk * sk[:, b:b + 1]
            out.append(accs[b] + prod.reshape(k_chunk // V7X_SUBLANES, V7X_SUBLANES, tn).sum(axis=0))
        return tuple(out)

    init = tuple(jnp.zeros((V7X_SUBLANES, tn), F32) for _ in range(nb))
    accs = lax.fori_loop(0, k_dim // k_chunk, body, init)
    row = lax.broadcasted_iota(jnp.int32, (nb, tn), 0)
    out = jnp.zeros((nb, tn), F32)
    for b in range(nb):
        out = jnp.where(row == b, accs[b].sum(axis=0, keepdims=True), out)
    o_ref[0] = out + b_ref[0]


def _adaln(c, w_ada, b_ada):
    depth, k_dim, n_dim = w_ada.shape
    nb = c.shape[0]
    tn = 512
    return pl.pallas_call(
        functools.partial(_adaln_kernel, k_chunk=64),
        out_shape=jax.ShapeDtypeStruct((depth, nb, n_dim), F32),
        grid=(depth, n_dim // tn),
        in_specs=[
            pl.BlockSpec((k_dim, nb), lambda l, j: (0, 0)),
            pl.BlockSpec((1, k_dim, tn), lambda l, j: (l, 0, j)),
            pl.BlockSpec((1, 1, tn), lambda l, j: (l, 0, j)),
        ],
        out_specs=pl.BlockSpec((1, nb, tn), lambda l, j: (l, 0, j)),
        scratch_shapes=[pltpu.VMEM((k_dim, nb), F32)],
        compiler_params=_params(("arbitrary", "arbitrary"), 2 * k_dim * tn * 4 + 16 * 2**20),
        name="adaln_mod",
    )(c.T, w_ada, b_ada.reshape(depth, 1, n_dim))


def _bias_tile_kernel(tab_ref, o_ref, *, tile, head_offset, dilated):
    h = pl.program_id(0) + head_offset
    d = pl.program_id(1)
    row = lax.broadcasted_iota(jnp.int32, (tile, tile), 0)
    col = lax.broadcasted_iota(jnp.int32, (tile, tile), 1)
    dist = d * tile + row - col
    val = jnp.full((tile, tile), tab_ref[NUM_BUCKETS - 1, h], F32)
    for b in range(NUM_BUCKETS - 2, -1, -1):
        val = jnp.where(dist < BUCKET_STARTS[b + 1], tab_ref[b, h], val)
    if dilated:
        valid = jnp.zeros((tile, tile), jnp.int32)
        for window, dil in DIL_PATTERNS:
            hit = (dist >= 0) & (dist <= window) & ((dist & (dil - 1)) == 0)
            valid = valid + hit.astype(jnp.int32)
        keep = valid > 0
    else:
        keep = dist >= 0
    o_ref[0, 0] = jnp.where(keep, val, NEG_INF)


def _bias_tiles(rel_bias, *, n_heads, head_offset, tile, n_tiles, dilated):
    return pl.pallas_call(
        functools.partial(_bias_tile_kernel, tile=tile, head_offset=head_offset, dilated=dilated),
        out_shape=jax.ShapeDtypeStruct((n_heads, n_tiles, tile, tile), F32),
        grid=(n_heads, n_tiles),
        in_specs=[pl.BlockSpec(memory_space=pltpu.SMEM)],
        out_specs=pl.BlockSpec((1, 1, tile, tile), lambda h, d: (h, d, 0, 0)),
        compiler_params=_params(("arbitrary", "arbitrary"), 16 * 2**20),
        name="bias_tiles_dil" if dilated else "bias_tiles_diff",
    )(rel_bias)


def _ln_mod_kernel(x_ref, sc_ref, sh_ref, u_ref):
    u = _ln_plain(x_ref[...]) * (1.0 + sc_ref[0]) + sh_ref[0]
    u_ref[...] = u.astype(u_ref.dtype)


def _res_ln_kernel(*refs, alpha, with_mod):
    if with_mod:
        x_ref, y_ref, gt_ref, g_ref, b_ref, sc_ref, sh_ref, xn_ref, u_ref = refs
    else:
        x_ref, y_ref, gt_ref, g_ref, b_ref, xn_ref = refs
    z = alpha * x_ref[...] + gt_ref[0] * y_ref[...]
    xn = _ln_plain(z) * g_ref[...] + b_ref[...]
    xn_ref[...] = xn
    if with_mod:
        u = _ln_plain(xn) * (1.0 + sc_ref[0]) + sh_ref[0]
        u_ref[...] = u.astype(u_ref.dtype)


ROW_TILE = 128


def _mod_spec(d_model, seq_blocks, row0, comp):
    return pl.BlockSpec((1, 1, d_model), lambda i: ((row0 + i // seq_blocks) * N_MOD + comp, 0, 0))


def _ln_mod(x, mod_rows, *, seq, row0, comp_scale, comp_shift):
    m, d = x.shape
    ts = ROW_TILE
    sb = seq // ts
    return pl.pallas_call(
        _ln_mod_kernel,
        out_shape=jax.ShapeDtypeStruct((m, d), BF16),
        grid=(m // ts,),
        in_specs=[pl.BlockSpec((ts, d), lambda i: (i, 0)),
                  _mod_spec(d, sb, row0, comp_scale),
                  _mod_spec(d, sb, row0, comp_shift)],
        out_specs=pl.BlockSpec((ts, d), lambda i: (i, 0)),
        compiler_params=_params(("arbitrary",), 12 * ts * d * 4 + 8 * 2**20),
        name="ln_mod",
    )(x, mod_rows, mod_rows)


def _res_ln(x, y, mod_rows, g, b, *, seq, alpha, row0, comp_gate, next_mod):
    m, d = x.shape
    ts = ROW_TILE
    sb = seq // ts
    row = pl.BlockSpec((ts, d), lambda i: (i, 0))
    vec = pl.BlockSpec((1, d), lambda i: (0, 0))
    in_specs = [row, row, _mod_spec(d, sb, row0, comp_gate), vec, vec]
    args = [x, y, mod_rows, g.reshape(1, d), b.reshape(1, d)]
    out_shape = [jax.ShapeDtypeStruct((m, d), F32)]
    out_specs = [row]
    if next_mod is not None:
        nrow0, ncs, nch = next_mod
        in_specs += [_mod_spec(d, sb, nrow0, ncs), _mod_spec(d, sb, nrow0, nch)]
        args += [mod_rows, mod_rows]
        out_shape.append(jax.ShapeDtypeStruct((m, d), BF16))
        out_specs.append(row)
    return pl.pallas_call(
        functools.partial(_res_ln_kernel, alpha=alpha, with_mod=next_mod is not None),
        out_shape=out_shape,
        grid=(m // ts,),
        in_specs=in_specs,
        out_specs=out_specs,
        compiler_params=_params(("arbitrary",), 16 * ts * d * 4 + 8 * 2**20),
        name="res_ln",
    )(*args)


def _matmul_kernel(x_ref, w_ref, o_ref):
    o_ref[...] = jnp.dot(x_ref[...], w_ref[...], preferred_element_type=F32).astype(o_ref.dtype)


def _matmul_tiles(m, k, n):
    tm = 1024 if k * 1024 * 2 <= 12 * 2**20 else 512
    tn = 512 if k * 512 * 2 <= 6 * 2**20 else V7X_MXU_DIM
    return min(tm, m), min(tn, n)


def _matmul(x, w, out_dtype):
    m, k = x.shape
    n = w.shape[1]
    tm, tn = _matmul_tiles(m, k, n)
    out_bytes = jnp.dtype(out_dtype).itemsize
    vmem = 2 * (tm * k * 2 + k * tn * 2 + tm * tn * out_bytes) + tm * tn * 4 + 8 * 2**20
    return pl.pallas_call(
        _matmul_kernel,
        out_shape=jax.ShapeDtypeStruct((m, n), out_dtype),
        grid=(m // tm, n // tn),
        in_specs=[pl.BlockSpec((tm, k), lambda i, j: (i, 0)),
                  pl.BlockSpec((k, tn), lambda i, j: (0, j))],
        out_specs=pl.BlockSpec((tm, tn), lambda i, j: (i, j)),
        compiler_params=_params(("arbitrary", "arbitrary"), vmem),
        name="matmul",
    )(x, w)


def _online_softmax_step(s, v, m_ref, l_ref, acc_ref, idx, weight=None):
    m_old = m_ref[idx]
    m_new = jnp.maximum(m_old, jnp.max(s, axis=-1, keepdims=True))
    p = jnp.exp(s - m_new)
    if weight is not None:
        p = p * weight
    alpha = jnp.exp(m_old - m_new)
    l_ref[idx] = alpha * l_ref[idx] + jnp.sum(p, axis=-1, keepdims=True)
    acc_ref[idx] = alpha * acc_ref[idx] + jnp.dot(p.astype(v.dtype), v, preferred_element_type=F32)
    m_ref[idx] = m_new


def _diff_attn_kernel(lam_ref, q_ref, k_ref, v_ref, bias_ref, g_ref, o_ref,
                      acc_ref, m_ref, l_ref, *, tile, n_bias, lam_init):
    i = pl.program_id(2)
    scale = HEAD_DIM ** -0.5
    m_ref[...] = jnp.full(m_ref.shape, NEG_INF, F32)
    l_ref[...] = jnp.zeros(l_ref.shape, F32)
    acc_ref[...] = jnp.zeros(acc_ref.shape, F32)

    def body(j, carry):
        off = pl.multiple_of(j * tile, tile)
        kj = k_ref[pl.ds(off, tile), :]
        vj = v_ref[pl.ds(off, tile), :]
        bt = bias_ref[0, jnp.minimum(i - j, n_bias - 1)]
        for mi in range(2):
            qm = q_ref[:, mi * HEAD_DIM:(mi + 1) * HEAD_DIM]
            km = kj[:, mi * HEAD_DIM:(mi + 1) * HEAD_DIM]
            s = lax.dot_general(qm, km, (((1,), (1,)), ((), ())), preferred_element_type=F32)
            _online_softmax_step(s * scale + bt, vj, m_ref, l_ref, acc_ref, mi)
        return carry

    lax.fori_loop(0, i + 1, body, 0)

    lv = lam_ref[...]
    lam = (jnp.exp(jnp.sum(lv[0:1] * lv[1:2], axis=-1, keepdims=True))
           - jnp.exp(jnp.sum(lv[2:3] * lv[3:4], axis=-1, keepdims=True)) + lam_init)
    o = acc_ref[0] * (1.0 / l_ref[0]) - lam * (acc_ref[1] * (1.0 / l_ref[1]))
    y = o * lax.rsqrt(jnp.mean(o * o, axis=-1, keepdims=True) + LN_EPS)
    o_ref[...] = ((y * g_ref[...]) * (1.0 - lam_init)).astype(o_ref.dtype)


def _diff_attn(qkv, lam_vecs, bias, g, *, batch, seq, n_heads, d_model, lam_init):
    m = batch * seq
    tile = DIFF_TILE
    nq = seq // tile
    hw = 2 * HEAD_DIM
    k_col0 = n_heads
    v_col0 = 2 * n_heads
    n_bias = bias.shape[1]
    vmem = (2 * (2 * seq * hw * 2 + n_bias * tile * tile * 4 + 2 * tile * hw * 2)
            + 8 * tile * tile * 4 + 4 * tile * hw * 4 + 8 * 2**20)
    return pl.pallas_call(
        functools.partial(_diff_attn_kernel, tile=tile, n_bias=n_bias, lam_init=lam_init),
        out_shape=jax.ShapeDtypeStruct((m, d_model), BF16),
        grid=(n_heads, batch, nq),
        in_specs=[
            pl.BlockSpec((4, HEAD_DIM), lambda h, b, i: (0, 0)),
            pl.BlockSpec((tile, hw), lambda h, b, i: (b * nq + i, h)),
            pl.BlockSpec((seq, hw), lambda h, b, i: (b, k_col0 + h)),
            pl.BlockSpec((seq, hw), lambda h, b, i: (b, v_col0 + h)),
            pl.BlockSpec((1, n_bias, tile, tile), lambda h, b, i: (h, 0, 0, 0)),
            pl.BlockSpec((1, hw), lambda h, b, i: (0, 0)),
        ],
        out_specs=pl.BlockSpec((tile, hw), lambda h, b, i: (b * nq + i, h)),
        scratch_shapes=[pltpu.VMEM((2, tile, hw), F32),
                        pltpu.VMEM((2, tile, 1), F32),
                        pltpu.VMEM((2, tile, 1), F32)],
        compiler_params=_params(("arbitrary", "arbitrary", "arbitrary"), vmem),
        name="diff_attn",
    )(lam_vecs, qkv, qkv, qkv, bias, g.reshape(1, hw))


def _dil_attn_kernel(q_ref, k_ref, v_ref, bias_ref, cnt_ref, g_ref, y_in_ref, o_ref,
                     acc_ref, m_ref, l_ref, *, tile, n_bias):
    del y_in_ref
    i = pl.program_id(2)
    scale = HEAD_DIM ** -0.5
    m_ref[...] = jnp.full(m_ref.shape, NEG_INF, F32)
    l_ref[...] = jnp.zeros(l_ref.shape, F32)
    acc_ref[...] = jnp.zeros(acc_ref.shape, F32)
    q = q_ref[...]

    def body(j, carry):
        off = pl.multiple_of(j * tile, tile)
        kj = k_ref[pl.ds(off, tile), :]
        vj = v_ref[pl.ds(off, tile), :]
        d = i - j
        s = lax.dot_general(q, kj, (((1,), (1,)), ((), ())), preferred_element_type=F32)
        _online_softmax_step(s * scale + bias_ref[0, d], vj, m_ref, l_ref, acc_ref, 0,
                             weight=cnt_ref[d])
        return carry

    lax.fori_loop(jnp.maximum(i - (n_bias - 1), 0), i + 1, body, 0)

    o = acc_ref[0] * (1.0 / l_ref[0])
    y = o * lax.rsqrt(jnp.mean(o * o, axis=-1, keepdims=True) + LN_EPS)
    o_ref[...] = (y * g_ref[...]).astype(o_ref.dtype)


def _dil_attn(qkv, y, bias, cnt, g, *, batch, seq, n_heads, q_col0, out_col0):
    m, d_model = y.shape
    tile = DIL_TILE
    nq = seq // tile
    n_bias = bias.shape[1]
    k_col0 = q_col0 + n_heads
    v_col0 = q_col0 + 2 * n_heads
    vmem = (2 * (2 * seq * HEAD_DIM * 2 + 2 * n_bias * tile * tile * 4) + 8 * tile * tile * 4
            + 8 * 2**20)
    return pl.pallas_call(
        functools.partial(_dil_attn_kernel, tile=tile, n_bias=n_bias),
        out_shape=jax.ShapeDtypeStruct((m, d_model), y.dtype),
        grid=(n_heads, batch, nq),
        in_specs=[
            pl.BlockSpec((tile, HEAD_DIM), lambda h, b, i: (b * nq + i, q_col0 + h)),
            pl.BlockSpec((seq, HEAD_DIM), lambda h, b, i: (b, k_col0 + h)),
            pl.BlockSpec((seq, HEAD_DIM), lambda h, b, i: (b, v_col0 + h)),
            pl.BlockSpec((1, n_bias, tile, tile), lambda h, b, i: (h, 0, 0, 0)),
            pl.BlockSpec((n_bias, tile, tile), lambda h, b, i: (0, 0, 0)),
            pl.BlockSpec((1, HEAD_DIM), lambda h, b, i: (0, 0)),
            pl.BlockSpec(memory_space=pl.ANY),
        ],
        out_specs=pl.BlockSpec((tile, HEAD_DIM), lambda h, b, i: (b * nq + i, out_col0 + h)),
        scratch_shapes=[pltpu.VMEM((1, tile, HEAD_DIM), F32),
                        pltpu.VMEM((1, tile, 1), F32),
                        pltpu.VMEM((1, tile, 1), F32)],
        input_output_aliases={6: 0},
        compiler_params=_params(("arbitrary", "arbitrary", "arbitrary"), vmem),
        name="dil_attn",
    )(qkv, qkv, qkv, bias, cnt, g.reshape(1, HEAD_DIM), y)


def _up_glu_kernel(x_ref, wg_ref, wv_ref, cw_ref, cb_ref, o_ref, hist_ref, *, tiles_per_seq):
    i = pl.program_id(0)
    j = pl.program_id(1)
    x = x_ref[...]
    g = jnp.dot(x, wg_ref[...], preferred_element_type=F32)
    v = jnp.dot(x, wv_ref[...], preferred_element_type=F32)
    tm = g.shape[0]

    @pl.when(i % tiles_per_seq == 0)
    def _():
        hist_ref[j] = jnp.zeros(hist_ref.shape[1:], F32)

    prev = hist_ref[j]
    hist_ref[j] = g[tm - V7X_SUBLANES:, :]
    row = lax.broadcasted_iota(jnp.int32, g.shape, 0)
    p1 = prev[V7X_SUBLANES - 1:V7X_SUBLANES, :]
    p2 = prev[V7X_SUBLANES - 2:V7X_SUBLANES - 1, :]
    g1 = jnp.where(row == 0, p1, pltpu.roll(g, 1, axis=0))
    g2 = jnp.where(row == 0, p2, jnp.where(row == 1, p1, pltpu.roll(g, 2, axis=0)))
    cw = cw_ref[...]
    gc = cw[2:3] * g + cw[1:2] * g1 + cw[0:1] * g2 + cb_ref[...]
    o_ref[...] = (gc * jax.nn.sigmoid(gc) * v).astype(o_ref.dtype)


def _up_glu(u, w_up, conv_w, conv_b, *, seq):
    m, k = u.shape
    f = w_up.shape[1] // 2
    tc = GLU_TILE
    nj = f // tc
    tm = min(1024, seq)
    vmem = 2 * (tm * k * 2 + 2 * k * tc * 2 + tm * tc * 2) + 8 * tm * tc * 4 + 8 * 2**20
    return pl.pallas_call(
        functools.partial(_up_glu_kernel, tiles_per_seq=seq // tm),
        out_shape=jax.ShapeDtypeStruct((m, f), BF16),
        grid=(m // tm, nj),
        in_specs=[
            pl.BlockSpec((tm, k), lambda i, j: (i, 0)),
            pl.BlockSpec((k, tc), lambda i, j: (0, j)),
            pl.BlockSpec((k, tc), lambda i, j: (0, nj + j)),
            pl.BlockSpec((CONV_WIDTH, tc), lambda i, j: (0, j)),
            pl.BlockSpec((1, tc), lambda i, j: (0, j)),
        ],
        out_specs=pl.BlockSpec((tm, tc), lambda i, j: (i, j)),
        scratch_shapes=[pltpu.VMEM((nj, V7X_SUBLANES, tc), F32)],
        compiler_params=_params(("arbitrary", "arbitrary"), vmem),
        name="up_glu",
    )(u, w_up, w_up, conv_w, conv_b.reshape(1, f))


def kernel(x, c, w_ada, b_ada, w_in, lambda_q1, lambda_k1, lambda_q2, lambda_k2, g_diff, g_dil,
           w_o, ln1_g, ln1_b, w_up, conv_w, conv_b, w_down, ln2_g, ln2_b, rel_bias):
    batch, seq, d_model = x.shape
    depth = w_ada.shape[0]
    m = batch * seq
    diff_width = d_model // 2
    n_diff = diff_width // (2 * HEAD_DIM)
    n_dil = (d_model - diff_width) // HEAD_DIM
    assert rel_bias.shape == (NUM_BUCKETS, n_diff + n_dil)
    assert seq % DIFF_TILE == 0 and seq % ROW_TILE == 0
    assert (w_up.shape[2] // 2) % GLU_TILE == 0
    alpha = (2 * depth) ** 0.25
    SH_A, SC_A, GT_A, SH_F, SC_F, GT_F = range(N_MOD)

    mod = _adaln(c, w_ada, b_ada)
    mod_rows = mod.reshape(depth * batch * N_MOD, 1, d_model)

    n_diff_tiles = -(-(BUCKET_STARTS[-1] + DIFF_TILE - 1) // DIFF_TILE) + 1
    n_dil_tiles = max(w for w, _ in DIL_PATTERNS) // DIL_TILE + 1
    assert (n_diff_tiles - 1) * DIFF_TILE - (DIFF_TILE - 1) >= BUCKET_STARTS[-1]
    diff_bias = _bias_tiles(rel_bias, n_heads=n_diff, head_offset=0, tile=DIFF_TILE,
                            n_tiles=n_diff_tiles, dilated=False)
    dil_bias = _bias_tiles(rel_bias, n_heads=n_dil, head_offset=n_diff, tile=DIL_TILE,
                           n_tiles=n_dil_tiles, dilated=True)
    t = np.arange(DIL_TILE)
    dist = (np.arange(n_dil_tiles)[:, None, None] * DIL_TILE + t[None, :, None] - t[None, None, :])
    dil_cnt = jnp.asarray(_dil_count_np(dist), F32)

    xf = x.reshape(m, d_model)
    u = _ln_mod(xf, mod_rows, seq=seq, row0=0, comp_scale=SC_A, comp_shift=SH_A)
    for l in range(depth):
        lam_init = 0.8 - 0.6 * math.exp(-0.3 * l)
        row0 = l * batch
        lam_vecs = jnp.stack([lambda_q1[l], lambda_k1[l], lambda_q2[l], lambda_k2[l]])
        qkv = _matmul(u, w_in[l].astype(BF16), BF16)
        y = _diff_attn(qkv, lam_vecs, diff_bias, g_diff[l], batch=batch, seq=seq,
                       n_heads=n_diff, d_model=d_model, lam_init=lam_init)
        y = _dil_attn(qkv, y, dil_bias, dil_cnt, g_dil[l], batch=batch, seq=seq, n_heads=n_dil,
                      q_col0=3 * diff_width // HEAD_DIM, out_col0=diff_width // HEAD_DIM)
        y = _matmul(y, w_o[l].astype(BF16), F32)
        xf, u = _res_ln(xf, y, mod_rows, ln1_g[l], ln1_b[l], seq=seq, alpha=alpha, row0=row0,
                        comp_gate=GT_A, next_mod=(row0, SC_F, SH_F))
        h = _up_glu(u, w_up[l].astype(BF16), conv_w[l], conv_b[l], seq=seq)
        y = _matmul(h, w_down[l].astype(BF16), F32)
        if l + 1 < depth:
            xf, u = _res_ln(xf, y, mod_rows, ln2_g[l], ln2_b[l], seq=seq, alpha=alpha, row0=row0,
                            comp_gate=GT_F, next_mod=(row0 + batch, SC_A, SH_A))
        else:
            (xf,) = _res_ln(xf, y, mod_rows, ln2_g[l], ln2_b[l], seq=seq, alpha=alpha, row0=row0,
                            comp_gate=GT_F, next_mod=None)
    return xf.reshape(batch, seq, d_model)
```

```python
import functools
import math

import numpy as np
import jax
import jax.numpy as jnp
from jax import lax
from jax.experimental import pallas as pl
from jax.experimental.pallas import tpu as pltpu

F32 = jnp.float32
BF16 = jnp.bfloat16

HEAD_DIM = 128
NUM_BUCKETS = 32
MAX_DISTANCE = 2048
DIL_PATTERNS = ((128, 1), (512, 4), (2048, 16))
CONV_WIDTH = 3
LN_EPS = 1e-5
NEG_INF = -1e30
N_MOD = 6

V7X_SUBLANES = 8
V7X_LANES = 128
V7X_MXU_DIM = 256
V7X_VMEM_LIMIT_CAP = 60 * 1024 * 1024

DIFF_TILE = 256
DIFF_GROUP = 4
DIL_TILE = 128
GLU_TILE = 256


def _bucket_starts():
    n = np.arange(MAX_DISTANCE + 1)
    max_exact = NUM_BUCKETS // 2
    nf = np.maximum(n, max_exact).astype(np.float64)
    large = max_exact + (np.log(nf / max_exact) / math.log(MAX_DISTANCE / max_exact)
                         * (NUM_BUCKETS - max_exact)).astype(np.int64)
    bucket = np.where(n < max_exact, n, np.minimum(large, NUM_BUCKETS - 1))
    assert np.all(np.diff(bucket) >= 0) and bucket[-1] == NUM_BUCKETS - 1
    return tuple(int(np.argmax(bucket >= b)) for b in range(NUM_BUCKETS))


BUCKET_STARTS = _bucket_starts()


def _dil_count_np(dist):
    cnt = np.zeros(dist.shape, np.int64)
    for window, dil in DIL_PATTERNS:
        cnt += (dist >= 0) & (dist <= window) & (dist % dil == 0)
    return cnt


def _params(semantics, vmem_bytes):
    return pltpu.CompilerParams(
        dimension_semantics=semantics,
        vmem_limit_bytes=int(min(vmem_bytes, V7X_VMEM_LIMIT_CAP)))


def _ln_plain(x):
    mu = jnp.mean(x, axis=-1, keepdims=True)
    xc = x - mu
    var = jnp.mean(xc * xc, axis=-1, keepdims=True)
    return xc * lax.rsqrt(var + LN_EPS)


def _adaln_kernel(ct_ref, w_ref, b_ref, o_ref, s_ref, *, k_chunk):
    k_dim, tn = w_ref.shape[1], w_ref.shape[2]
    nb = ct_ref.shape[1]

    @pl.when((pl.program_id(0) == 0) & (pl.program_id(1) == 0))
    def _():
        ct = ct_ref[...]
        s = ct * jax.nn.sigmoid(ct)
        for b in range(nb):
            s_ref[b] = jnp.broadcast_to(s[:, b:b + 1], (k_dim, V7X_LANES))

    def body(t, accs):
        k0 = pl.multiple_of(t * k_chunk, k_chunk)
        accs = list(accs)
        for r in range(k_chunk // V7X_SUBLANES):
            rows = pl.ds(k0 + r * V7X_SUBLANES, V7X_SUBLANES)
            wr = w_ref[0, rows, :]
            for b in range(nb):
                sb = jnp.concatenate([s_ref[b, rows, :]] * (tn // V7X_LANES), axis=1)
                accs[b] = accs[b] + wr * sb
        return tuple(accs)

    init = tuple(jnp.zeros((V7X_SUBLANES, tn), F32) for _ in range(nb))
    accs = lax.fori_loop(0, k_dim // k_chunk, body, init)
    row = lax.broadcasted_iota(jnp.int32, (nb, tn), 0)
    out = jnp.zeros((nb, tn), F32)
    for b in range(nb):
        out = jnp.where(row == b, accs[b].sum(axis=0, keepdims=True), out)
    o_ref[0] = out + b_ref[0]


def _adaln(c, w_ada, b_ada):
    depth, k_dim, n_dim = w_ada.shape
    nb = c.shape[0]
    tn = 512
    return pl.pallas_call(
        functools.partial(_adaln_kernel, k_chunk=64),
        out_shape=jax.ShapeDtypeStruct((depth, nb, n_dim), F32),
        grid=(depth, n_dim // tn),
        in_specs=[
            pl.BlockSpec((k_dim, nb), lambda l, j: (0, 0)),
            pl.BlockSpec((1, k_dim, tn), lambda l, j: (l, 0, j)),
            pl.BlockSpec((1, 1, tn), lambda l, j: (l, 0, j)),
        ],
        out_specs=pl.BlockSpec((1, nb, tn), lambda l, j: (l, 0, j)),
        scratch_shapes=[pltpu.VMEM((nb, k_dim, V7X_LANES), F32)],
        compiler_params=_params(("arbitrary", "arbitrary"),
                                2 * k_dim * tn * 4 + 2 * nb * k_dim * V7X_LANES * 4 + 16 * 2**20),
        name="adaln_mod",
    )(c.T, w_ada, b_ada.reshape(depth, 1, n_dim))


def _bias_tile_kernel(tab_ref, o_ref, *, tile, head_offset, dilated):
    h = pl.program_id(0) + head_offset
    d = pl.program_id(1)
    row = lax.broadcasted_iota(jnp.int32, (tile, tile), 0)
    col = lax.broadcasted_iota(jnp.int32, (tile, tile), 1)
    dist = d * tile + row - col
    val = jnp.full((tile, tile), tab_ref[NUM_BUCKETS - 1, h], F32)
    for b in range(NUM_BUCKETS - 2, -1, -1):
        val = jnp.where(dist < BUCKET_STARTS[b + 1], tab_ref[b, h], val)
    if dilated:
        valid = jnp.zeros((tile, tile), jnp.int32)
        for window, dil in DIL_PATTERNS:
            hit = (dist >= 0) & (dist <= window) & ((dist & (dil - 1)) == 0)
            valid = valid + hit.astype(jnp.int32)
        keep = valid > 0
    else:
        keep = dist >= 0
    keep = keep & (d < pl.num_programs(1) - 1)
    o_ref[0, 0] = jnp.where(keep, val, NEG_INF)


def _bias_tiles(rel_bias, *, n_heads, head_offset, tile, n_tiles, dilated):
    return pl.pallas_call(
        functools.partial(_bias_tile_kernel, tile=tile, head_offset=head_offset, dilated=dilated),
        out_shape=jax.ShapeDtypeStruct((n_heads, n_tiles + 1, tile, tile), F32),
        grid=(n_heads, n_tiles + 1),
        in_specs=[pl.BlockSpec(memory_space=pltpu.SMEM)],
        out_specs=pl.BlockSpec((1, 1, tile, tile), lambda h, d: (h, d, 0, 0)),
        compiler_params=_params(("arbitrary", "arbitrary"), 16 * 2**20),
        name="bias_tiles_dil" if dilated else "bias_tiles_diff",
    )(rel_bias)


def _ln_mod_kernel(x_ref, sc_ref, sh_ref, u_ref):
    u = _ln_plain(x_ref[...]) * (1.0 + sc_ref[0]) + sh_ref[0]
    u_ref[...] = u.astype(u_ref.dtype)


def _res_ln_kernel(*refs, alpha, with_mod):
    if with_mod:
        x_ref, y_ref, gt_ref, g_ref, b_ref, sc_ref, sh_ref, xn_ref, u_ref = refs
    else:
        x_ref, y_ref, gt_ref, g_ref, b_ref, xn_ref = refs
    z = alpha * x_ref[...] + gt_ref[0] * y_ref[...]
    xn = _ln_plain(z) * g_ref[...] + b_ref[...]
    xn_ref[...] = xn
    if with_mod:
        u = _ln_plain(xn) * (1.0 + sc_ref[0]) + sh_ref[0]
        u_ref[...] = u.astype(u_ref.dtype)


ROW_TILE = 128


def _mod_spec(d_model, seq_blocks, row0, comp):
    return pl.BlockSpec((1, 1, d_model), lambda i: ((row0 + i // seq_blocks) * N_MOD + comp, 0, 0))


def _ln_mod(x, mod_rows, *, seq, row0, comp_scale, comp_shift):
    m, d = x.shape
    ts = ROW_TILE
    sb = seq // ts
    return pl.pallas_call(
        _ln_mod_kernel,
        out_shape=jax.ShapeDtypeStruct((m, d), BF16),
        grid=(m // ts,),
        in_specs=[pl.BlockSpec((ts, d), lambda i: (i, 0)),
                  _mod_spec(d, sb, row0, comp_scale),
                  _mod_spec(d, sb, row0, comp_shift)],
        out_specs=pl.BlockSpec((ts, d), lambda i: (i, 0)),
        compiler_params=_params(("arbitrary",), 12 * ts * d * 4 + 8 * 2**20),
        name="ln_mod",
    )(x, mod_rows, mod_rows)


def _res_ln(x, y, mod_rows, g, b, *, seq, alpha, row0, comp_gate, next_mod):
    m, d = x.shape
    ts = ROW_TILE
    sb = seq // ts
    row = pl.BlockSpec((ts, d), lambda i: (i, 0))
    vec = pl.BlockSpec((1, d), lambda i: (0, 0))
    in_specs = [row, row, _mod_spec(d, sb, row0, comp_gate), vec, vec]
    args = [x, y, mod_rows, g.reshape(1, d), b.reshape(1, d)]
    out_shape = [jax.ShapeDtypeStruct((m, d), F32)]
    out_specs = [row]
    if next_mod is not None:
        nrow0, ncs, nch = next_mod
        in_specs += [_mod_spec(d, sb, nrow0, ncs), _mod_spec(d, sb, nrow0, nch)]
        args += [mod_rows, mod_rows]
        out_shape.append(jax.ShapeDtypeStruct((m, d), BF16))
        out_specs.append(row)
    return pl.pallas_call(
        functools.partial(_res_ln_kernel, alpha=alpha, with_mod=next_mod is not None),
        out_shape=out_shape,
        grid=(m // ts,),
        in_specs=in_specs,
        out_specs=out_specs,
        compiler_params=_params(("arbitrary",), 16 * ts * d * 4 + 8 * 2**20),
        name="res_ln",
    )(*args)


def _matmul_kernel(x_ref, w_ref, o_ref):
    w = w_ref[...].astype(BF16)
    o_ref[...] = jnp.dot(x_ref[...], w, preferred_element_type=F32).astype(o_ref.dtype)


def _matmul_tiles(m, k, n, w_bytes):
    tm = 1024 if k * 1024 * 2 <= 12 * 2**20 else 512
    tn = 512 if k * 512 * w_bytes <= 8 * 2**20 else V7X_MXU_DIM
    return min(tm, m), min(tn, n)


def _matmul(x, w, layer, out_dtype):
    m, k = x.shape
    n = w.shape[2]
    w_bytes = jnp.dtype(w.dtype).itemsize
    tm, tn = _matmul_tiles(m, k, n, w_bytes)
    out_bytes = jnp.dtype(out_dtype).itemsize
    vmem = (2 * (tm * k * 2 + k * tn * w_bytes + tm * tn * out_bytes)
            + k * tn * 2 + tm * tn * 4 + 8 * 2**20)
    return pl.pallas_call(
        _matmul_kernel,
        out_shape=jax.ShapeDtypeStruct((m, n), out_dtype),
        grid=(m // tm, n // tn),
        in_specs=[pl.BlockSpec((tm, k), lambda i, j: (i, 0)),
                  pl.BlockSpec((None, k, tn), lambda i, j: (layer, 0, j))],
        out_specs=pl.BlockSpec((tm, tn), lambda i, j: (i, j)),
        compiler_params=_params(("arbitrary", "arbitrary"), vmem),
        name="matmul",
    )(x, w)


def _lane_tile(x, width):
    return jnp.concatenate([x] * (width // V7X_LANES), axis=1)


def _lane_fold(p):
    out = p[:, :V7X_LANES]
    for c in range(1, p.shape[1] // V7X_LANES):
        out = out + p[:, c * V7X_LANES:(c + 1) * V7X_LANES]
    return out


def _dot_nt(a, b):
    return lax.dot_general(a, b, (((1,), (1,)), ((), ())), preferred_element_type=F32)


def _diff_attn_kernel(lam_ref, q_ref, k_ref, v_ref, bias_ref, g_ref, o_ref,
                      acc_ref, m_ref, l_ref, *, tile, group, n_bias, lam_init):
    i = pl.program_id(2)
    scale = HEAD_DIM ** -0.5
    width = group * tile
    m_ref[...] = jnp.full(m_ref.shape, NEG_INF, F32)
    l_ref[...] = jnp.zeros(l_ref.shape, F32)
    acc_ref[...] = jnp.zeros(acc_ref.shape, F32)

    def body(gi, carry):
        j0 = gi * group
        rows = pl.ds(pl.multiple_of(j0 * tile, width), width)
        vg = v_ref[rows, :]
        bts = []
        for t in range(group):
            d = i - (j0 + t)
            bts.append(bias_ref[0, jnp.where(d >= 0, jnp.minimum(d, n_bias - 2), n_bias - 1)])
        bt = jnp.concatenate(bts, axis=1)
        for mi in range(2):
            cols = slice(mi * HEAD_DIM, (mi + 1) * HEAD_DIM)
            s = _dot_nt(q_ref[:, cols], k_ref[rows, cols]) * scale + bt
            m_old = m_ref[mi]
            m_new = jnp.maximum(m_old, jnp.max(s, axis=-1, keepdims=True))
            p = jnp.exp(s - _lane_tile(m_new, width))
            alpha = jnp.exp(m_old - m_new)
            l_ref[mi] = alpha * l_ref[mi] + _lane_fold(p)
            acc_ref[mi] = (_lane_tile(alpha, acc_ref.shape[2]) * acc_ref[mi]
                           + jnp.dot(p.astype(vg.dtype), vg, preferred_element_type=F32))
            m_ref[mi] = m_new
        return carry

    lax.fori_loop(0, i // group + 1, body, 0)

    lv = lam_ref[...]
    lam = (jnp.exp(jnp.sum(lv[0:1] * lv[1:2], axis=-1, keepdims=True))
           - jnp.exp(jnp.sum(lv[2:3] * lv[3:4], axis=-1, keepdims=True)) + lam_init)
    l0 = jnp.sum(l_ref[0], axis=-1, keepdims=True)
    l1 = jnp.sum(l_ref[1], axis=-1, keepdims=True)
    o = acc_ref[0] * (1.0 / l0) - lam * (acc_ref[1] * (1.0 / l1))
    y = o * lax.rsqrt(jnp.mean(o * o, axis=-1, keepdims=True) + LN_EPS)
    o_ref[...] = ((y * g_ref[...]) * (1.0 - lam_init)).astype(o_ref.dtype)


def _diff_attn(qkv, lam_vecs, bias, g, *, batch, seq, n_heads, d_model, lam_init):
    m = batch * seq
    tile = DIFF_TILE
    group = DIFF_GROUP
    nq = seq // tile
    hw = 2 * HEAD_DIM
    k_col0 = n_heads
    v_col0 = 2 * n_heads
    n_bias = bias.shape[1]
    vmem = (2 * (2 * seq * hw * 2 + n_bias * tile * tile * 4 + 2 * tile * hw * 2)
            + 6 * tile * group * tile * 4 + 4 * tile * hw * 4 + 8 * 2**20)
    return pl.pallas_call(
        functools.partial(_diff_attn_kernel, tile=tile, group=group, n_bias=n_bias,
                          lam_init=lam_init),
        out_shape=jax.ShapeDtypeStruct((m, d_model), BF16),
        grid=(n_heads, batch, nq),
        in_specs=[
            pl.BlockSpec((4, HEAD_DIM), lambda h, b, i: (0, 0)),
            pl.BlockSpec((tile, hw), lambda h, b, i: (b * nq + i, h)),
            pl.BlockSpec((seq, hw), lambda h, b, i: (b, k_col0 + h)),
            pl.BlockSpec((seq, hw), lambda h, b, i: (b, v_col0 + h)),
            pl.BlockSpec((1, n_bias, tile, tile), lambda h, b, i: (h, 0, 0, 0)),
            pl.BlockSpec((1, hw), lambda h, b, i: (0, 0)),
        ],
        out_specs=pl.BlockSpec((tile, hw), lambda h, b, i: (b * nq + i, h)),
        scratch_shapes=[pltpu.VMEM((2, tile, hw), F32),
                        pltpu.VMEM((2, tile, V7X_LANES), F32),
                        pltpu.VMEM((2, tile, V7X_LANES), F32)],
        compiler_params=_params(("arbitrary", "arbitrary", "arbitrary"), vmem),
        name="diff_attn",
    )(lam_vecs, qkv, qkv, qkv, bias, g.reshape(1, hw))


def _dil_attn_kernel(q_ref, k_ref, v_ref, bias_ref, cnt_ref, g_ref, y_in_ref, o_ref, *, tile, n_win):
    del y_in_ref
    i = pl.program_id(2)
    scale = HEAD_DIM ** -0.5
    kt0 = jnp.maximum(i - (n_win - 1), 0)
    rows = pl.ds(pl.multiple_of(kt0 * tile, tile), n_win * tile)
    idx = []
    for t in range(n_win):
        d = i - (kt0 + t)
        idx.append(jnp.where(d >= 0, d, n_win))
    bias = jnp.concatenate([bias_ref[0, ix] for ix in idx], axis=1)
    s = _dot_nt(q_ref[...], k_ref[rows, :]) * scale + bias
    m = jnp.max(s, axis=-1, keepdims=True)
    p = jnp.exp(s - m) * jnp.concatenate([cnt_ref[ix] for ix in idx], axis=1)
    l = jnp.sum(p, axis=-1, keepdims=True)
    vw = v_ref[rows, :]
    o = jnp.dot(p.astype(vw.dtype), vw, preferred_element_type=F32) * (1.0 / l)
    y = o * lax.rsqrt(jnp.mean(o * o, axis=-1, keepdims=True) + LN_EPS)
    o_ref[...] = (y * g_ref[...]).astype(o_ref.dtype)


def _dil_attn(qkv, y, bias, cnt, g, *, batch, seq, n_heads, q_col0, out_col0):
    m, d_model = y.shape
    tile = DIL_TILE
    nq = seq // tile
    n_bias = bias.shape[1]
    n_win = n_bias - 1
    assert seq >= n_win * tile
    k_col0 = q_col0 + n_heads
    v_col0 = q_col0 + 2 * n_heads
    vmem = (2 * (2 * seq * HEAD_DIM * 2 + 2 * n_bias * tile * tile * 4)
            + 8 * tile * n_win * tile * 4 + 8 * 2**20)
    return pl.pallas_call(
        functools.partial(_dil_attn_kernel, tile=tile, n_win=n_win),
        out_shape=jax.ShapeDtypeStruct((m, d_model), y.dtype),
        grid=(n_heads, batch, nq),
        in_specs=[
            pl.BlockSpec((tile, HEAD_DIM), lambda h, b, i: (b * nq + i, q_col0 + h)),
            pl.BlockSpec((seq, HEAD_DIM), lambda h, b, i: (b, k_col0 + h)),
            pl.BlockSpec((seq, HEAD_DIM), lambda h, b, i: (b, v_col0 + h)),
            pl.BlockSpec((1, n_bias, tile, tile), lambda h, b, i: (h, 0, 0, 0)),
            pl.BlockSpec((n_bias, tile, tile), lambda h, b, i: (0, 0, 0)),
            pl.BlockSpec((1, HEAD_DIM), lambda h, b, i: (0, 0)),
            pl.BlockSpec(memory_space=pl.ANY),
        ],
        out_specs=pl.BlockSpec((tile, HEAD_DIM), lambda h, b, i: (b * nq + i, out_col0 + h)),
        input_output_aliases={6: 0},
        compiler_params=_params(("arbitrary", "arbitrary", "arbitrary"), vmem),
        name="dil_attn",
    )(qkv, qkv, qkv, bias, cnt, g.reshape(1, HEAD_DIM), y)


def _up_glu_kernel(x_ref, wg_ref, wv_ref, cw_ref, cb_ref, o_ref, hist_ref, *, tiles_per_seq):
    i = pl.program_id(0)
    j = pl.program_id(1)
    x = x_ref[...]
    g = jnp.dot(x, wg_ref[...].astype(BF16), preferred_element_type=F32)
    v = jnp.dot(x, wv_ref[...].astype(BF16), preferred_element_type=F32)
    tm = g.shape[0]

    @pl.when(i % tiles_per_seq == 0)
    def _():
        hist_ref[j] = jnp.zeros(hist_ref.shape[1:], F32)

    prev = hist_ref[j]
    hist_ref[j] = g[tm - V7X_SUBLANES:, :]
    row = lax.broadcasted_iota(jnp.int32, g.shape, 0)
    p1 = prev[V7X_SUBLANES - 1:V7X_SUBLANES, :]
    p2 = prev[V7X_SUBLANES - 2:V7X_SUBLANES - 1, :]
    g1 = jnp.where(row == 0, p1, pltpu.roll(g, 1, axis=0))
    g2 = jnp.where(row == 0, p2, jnp.where(row == 1, p1, pltpu.roll(g, 2, axis=0)))
    cw = cw_ref[...]
    gc = cw[2:3] * g + cw[1:2] * g1 + cw[0:1] * g2 + cb_ref[...]
    o_ref[...] = (gc * jax.nn.sigmoid(gc) * v).astype(o_ref.dtype)


def _up_glu(u, w_up, layer, conv_w, conv_b, *, seq):
    m, k = u.shape
    f = w_up.shape[2] // 2
    w_bytes = jnp.dtype(w_up.dtype).itemsize
    tc = GLU_TILE
    nj = f // tc
    tm = min(1024, seq)
    vmem = (2 * (tm * k * 2 + 2 * k * tc * w_bytes + tm * tc * 2) + 2 * k * tc * 2
            + 8 * tm * tc * 4 + 8 * 2**20)
    return pl.pallas_call(
        functools.partial(_up_glu_kernel, tiles_per_seq=seq // tm),
        out_shape=jax.ShapeDtypeStruct((m, f), BF16),
        grid=(m // tm, nj),
        in_specs=[
            pl.BlockSpec((tm, k), lambda i, j: (i, 0)),
            pl.BlockSpec((None, k, tc), lambda i, j: (layer, 0, j)),
            pl.BlockSpec((None, k, tc), lambda i, j: (layer, 0, nj + j)),
            pl.BlockSpec((CONV_WIDTH, tc), lambda i, j: (0, j)),
            pl.BlockSpec((1, tc), lambda i, j: (0, j)),
        ],
        out_specs=pl.BlockSpec((tm, tc), lambda i, j: (i, j)),
        scratch_shapes=[pltpu.VMEM((nj, V7X_SUBLANES, tc), F32)],
        compiler_params=_params(("arbitrary", "arbitrary"), vmem),
        name="up_glu",
    )(u, w_up, w_up, conv_w, conv_b.reshape(1, f))


def kernel(x, c, w_ada, b_ada, w_in, lambda_q1, lambda_k1, lambda_q2, lambda_k2, g_diff, g_dil,
           w_o, ln1_g, ln1_b, w_up, conv_w, conv_b, w_down, ln2_g, ln2_b, rel_bias):
    batch, seq, d_model = x.shape
    depth = w_ada.shape[0]
    m = batch * seq
    diff_width = d_model // 2
    n_diff = diff_width // (2 * HEAD_DIM)
    n_dil = (d_model - diff_width) // HEAD_DIM
    assert rel_bias.shape == (NUM_BUCKETS, n_diff + n_dil)
    assert seq % (DIFF_TILE * DIFF_GROUP) == 0 and seq % ROW_TILE == 0
    assert (w_up.shape[2] // 2) % GLU_TILE == 0
    alpha = (2 * depth) ** 0.25
    SH_A, SC_A, GT_A, SH_F, SC_F, GT_F = range(N_MOD)

    mod = _adaln(c, w_ada, b_ada)
    mod_rows = mod.reshape(depth * batch * N_MOD, 1, d_model)

    n_diff_tiles = -(-(BUCKET_STARTS[-1] + DIFF_TILE - 1) // DIFF_TILE) + 1
    n_dil_tiles = max(w for w, _ in DIL_PATTERNS) // DIL_TILE + 1
    assert (n_diff_tiles - 1) * DIFF_TILE - (DIFF_TILE - 1) >= BUCKET_STARTS[-1]
    diff_bias = _bias_tiles(rel_bias, n_heads=n_diff, head_offset=0, tile=DIFF_TILE,
                            n_tiles=n_diff_tiles, dilated=False)
    dil_bias = _bias_tiles(rel_bias, n_heads=n_dil, head_offset=n_diff, tile=DIL_TILE,
                           n_tiles=n_dil_tiles, dilated=True)
    t = np.arange(DIL_TILE)
    dist = (np.arange(n_dil_tiles)[:, None, None] * DIL_TILE + t[None, :, None] - t[None, None, :])
    cnt = np.concatenate([_dil_count_np(dist), np.zeros((1, DIL_TILE, DIL_TILE), np.int64)])
    dil_cnt = jnp.asarray(cnt, F32)
    w_down_bf16 = w_down.astype(BF16)

    xf = x.reshape(m, d_model)
    u = _ln_mod(xf, mod_rows, seq=seq, row0=0, comp_scale=SC_A, comp_shift=SH_A)
    for l in range(depth):
        lam_init = 0.8 - 0.6 * math.exp(-0.3 * l)
        row0 = l * batch
        lam_vecs = jnp.stack([lambda_q1[l], lambda_k1[l], lambda_q2[l], lambda_k2[l]])
        qkv = _matmul(u, w_in, l, BF16)
        y = _diff_attn(qkv, lam_vecs, diff_bias, g_diff[l], batch=batch, seq=seq,
                       n_heads=n_diff, d_model=d_model, lam_init=lam_init)
        y = _dil_attn(qkv, y, dil_bias, dil_cnt, g_dil[l], batch=batch, seq=seq, n_heads=n_dil,
                      q_col0=3 * diff_width // HEAD_DIM, out_col0=diff_width // HEAD_DIM)
        y = _matmul(y, w_o, l, F32)
        xf, u = _res_ln(xf, y, mod_rows, ln1_g[l], ln1_b[l], seq=seq, alpha=alpha, row0=row0,
                        comp_gate=GT_A, next_mod=(row0, SC_F, SH_F))
        h = _up_glu(u, w_up, l, conv_w[l], conv_b[l], seq=seq)
        y = _matmul(h, w_down_bf16, l, F32)
        if l + 1 < depth:
            xf, u = _res_ln(xf, y, mod_rows, ln2_g[l], ln2_b[l], seq=seq, alpha=alpha, row0=row0,
                            comp_gate=GT_F, next_mod=(row0 + batch, SC_A, SH_A))
        else:
            (xf,) = _res_ln(xf, y, mod_rows, ln2_g[l], ln2_b[l], seq=seq, alpha=alpha, row0=row0,
                            comp_gate=GT_F, next_mod=None)
    return xf.reshape(batch, seq, d_model)
```

```python
import functools
import math

import numpy as np
import jax
import jax.numpy as jnp
from jax import lax
from jax.experimental import pallas as pl
from jax.experimental.pallas import tpu as pltpu

F32 = jnp.float32
BF16 = jnp.bfloat16

HEAD_DIM = 128
NUM_BUCKETS = 32
MAX_DISTANCE = 2048
DIL_PATTERNS = ((128, 1), (512, 4), (2048, 16))
CONV_WIDTH = 3
LN_EPS = 1e-5
NEG_INF = -1e30
N_MOD = 6

V7X_SUBLANES = 8
V7X_LANES = 128
V7X_MXU_DIM = 256
V7X_VMEM_LIMIT_CAP = 60 * 1024 * 1024

DIFF_TILE = 256
DIFF_GROUP = 4
DIFF_HEADS_PER_STEP = 2
DIL_TILE = 128
DIL_HEADS_PER_STEP = 4
GLU_TILE = 256
GLU_ROW_CHUNKS = 4


def _bucket_starts():
    n = np.arange(MAX_DISTANCE + 1)
    max_exact = NUM_BUCKETS // 2
    nf = np.maximum(n, max_exact).astype(np.float64)
    large = max_exact + (np.log(nf / max_exact) / math.log(MAX_DISTANCE / max_exact)
                         * (NUM_BUCKETS - max_exact)).astype(np.int64)
    bucket = np.where(n < max_exact, n, np.minimum(large, NUM_BUCKETS - 1))
    assert np.all(np.diff(bucket) >= 0) and bucket[-1] == NUM_BUCKETS - 1
    return tuple(int(np.argmax(bucket >= b)) for b in range(NUM_BUCKETS))


BUCKET_STARTS = _bucket_starts()


LOG2E = math.log2(math.e)


def _params(semantics, vmem_bytes):
    return pltpu.CompilerParams(
        dimension_semantics=semantics,
        vmem_limit_bytes=int(min(vmem_bytes, V7X_VMEM_LIMIT_CAP)))


def _ln_plain(x):
    mu = jnp.mean(x, axis=-1, keepdims=True)
    xc = x - mu
    var = jnp.mean(xc * xc, axis=-1, keepdims=True)
    return xc * lax.rsqrt(var + LN_EPS)


def _adaln_kernel(ct_ref, w_ref, b_ref, o_ref, s_ref, *, k_chunk):
    k_dim, tn = w_ref.shape[1], w_ref.shape[2]
    nb = ct_ref.shape[1]

    @pl.when((pl.program_id(0) == 0) & (pl.program_id(1) == 0))
    def _():
        ct = ct_ref[...]
        s = ct * jax.nn.sigmoid(ct)
        for b in range(nb):
            s_ref[b] = jnp.broadcast_to(s[:, b:b + 1], (k_dim, V7X_LANES))

    def body(t, accs):
        k0 = pl.multiple_of(t * k_chunk, k_chunk)
        accs = list(accs)
        for r in range(k_chunk // V7X_SUBLANES):
            rows = pl.ds(k0 + r * V7X_SUBLANES, V7X_SUBLANES)
            wr = w_ref[0, rows, :]
            for b in range(nb):
                sb = jnp.concatenate([s_ref[b, rows, :]] * (tn // V7X_LANES), axis=1)
                accs[b] = accs[b] + wr * sb
        return tuple(accs)

    init = tuple(jnp.zeros((V7X_SUBLANES, tn), F32) for _ in range(nb))
    accs = lax.fori_loop(0, k_dim // k_chunk, body, init)
    row = lax.broadcasted_iota(jnp.int32, (nb, tn), 0)
    out = jnp.zeros((nb, tn), F32)
    for b in range(nb):
        out = jnp.where(row == b, accs[b].sum(axis=0, keepdims=True), out)
    o_ref[0] = out + b_ref[0]


def _adaln(c, w_ada, b_ada):
    depth, k_dim, n_dim = w_ada.shape
    nb = c.shape[0]
    tn = 512
    return pl.pallas_call(
        functools.partial(_adaln_kernel, k_chunk=64),
        out_shape=jax.ShapeDtypeStruct((depth, nb, n_dim), F32),
        grid=(depth, n_dim // tn),
        in_specs=[
            pl.BlockSpec((k_dim, nb), lambda l, j: (0, 0)),
            pl.BlockSpec((1, k_dim, tn), lambda l, j: (l, 0, j)),
            pl.BlockSpec((1, 1, tn), lambda l, j: (l, 0, j)),
        ],
        out_specs=pl.BlockSpec((1, nb, tn), lambda l, j: (l, 0, j)),
        scratch_shapes=[pltpu.VMEM((nb, k_dim, V7X_LANES), F32)],
        compiler_params=_params(("arbitrary", "arbitrary"),
                                2 * k_dim * tn * 4 + 2 * nb * k_dim * V7X_LANES * 4 + 16 * 2**20),
        name="adaln_mod",
    )(c.T, w_ada, b_ada.reshape(depth, 1, n_dim))


def _bias_tile_kernel(tab_ref, o_ref, *, tile, head_offset, dilated):
    h = pl.program_id(0) + head_offset
    d = pl.program_id(1)
    row = lax.broadcasted_iota(jnp.int32, (tile, tile), 0)
    col = lax.broadcasted_iota(jnp.int32, (tile, tile), 1)
    dist = d * tile + row - col
    val = jnp.full((tile, tile), tab_ref[NUM_BUCKETS - 1, h], F32)
    for b in range(NUM_BUCKETS - 2, -1, -1):
        val = jnp.where(dist < BUCKET_STARTS[b + 1], tab_ref[b, h], val)
    val = val * LOG2E
    if dilated:
        count = jnp.zeros((tile, tile), jnp.int32)
        for window, dil in DIL_PATTERNS:
            hit = (dist >= 0) & (dist <= window) & ((dist & (dil - 1)) == 0)
            count = count + hit.astype(jnp.int32)
        for n in range(2, len(DIL_PATTERNS) + 1):
            val = jnp.where(count == n, val + math.log2(n), val)
        keep = count > 0
    else:
        keep = dist >= 0
    keep = keep & (d < pl.num_programs(1) - 1)
    o_ref[0, 0] = jnp.where(keep, val, NEG_INF)


def _bias_tiles(rel_bias, *, n_heads, head_offset, tile, n_tiles, dilated):
    return pl.pallas_call(
        functools.partial(_bias_tile_kernel, tile=tile, head_offset=head_offset, dilated=dilated),
        out_shape=jax.ShapeDtypeStruct((n_heads, n_tiles + 1, tile, tile), F32),
        grid=(n_heads, n_tiles + 1),
        in_specs=[pl.BlockSpec(memory_space=pltpu.SMEM)],
        out_specs=pl.BlockSpec((1, 1, tile, tile), lambda h, d: (h, d, 0, 0)),
        compiler_params=_params(("arbitrary", "arbitrary"), 16 * 2**20),
        name="bias_tiles_dil" if dilated else "bias_tiles_diff",
    )(rel_bias)


def _ln_mod_kernel(x_ref, sc_ref, sh_ref, u_ref):
    u = _ln_plain(x_ref[...]) * (1.0 + sc_ref[0]) + sh_ref[0]
    u_ref[...] = u.astype(u_ref.dtype)


def _res_ln_kernel(*refs, alpha, with_mod):
    if with_mod:
        x_ref, y_ref, gt_ref, g_ref, b_ref, sc_ref, sh_ref, xn_ref, u_ref = refs
    else:
        x_ref, y_ref, gt_ref, g_ref, b_ref, xn_ref = refs
    z = alpha * x_ref[...] + gt_ref[0] * y_ref[...].astype(F32)
    xn = _ln_plain(z) * g_ref[...] + b_ref[...]
    xn_ref[...] = xn
    if with_mod:
        u = _ln_plain(xn) * (1.0 + sc_ref[0]) + sh_ref[0]
        u_ref[...] = u.astype(u_ref.dtype)


ROW_TILE = 128


def _mod_spec(d_model, seq_blocks, row0, comp):
    return pl.BlockSpec((1, 1, d_model), lambda i: ((row0 + i // seq_blocks) * N_MOD + comp, 0, 0))


def _ln_mod(x, mod_rows, *, seq, row0, comp_scale, comp_shift):
    m, d = x.shape
    ts = ROW_TILE
    sb = seq // ts
    return pl.pallas_call(
        _ln_mod_kernel,
        out_shape=jax.ShapeDtypeStruct((m, d), BF16),
        grid=(m // ts,),
        in_specs=[pl.BlockSpec((ts, d), lambda i: (i, 0)),
                  _mod_spec(d, sb, row0, comp_scale),
                  _mod_spec(d, sb, row0, comp_shift)],
        out_specs=pl.BlockSpec((ts, d), lambda i: (i, 0)),
        compiler_params=_params(("arbitrary",), 12 * ts * d * 4 + 8 * 2**20),
        name="ln_mod",
    )(x, mod_rows, mod_rows)


def _res_ln(x, y, mod_rows, g, b, *, seq, alpha, row0, comp_gate, next_mod):
    m, d = x.shape
    ts = ROW_TILE
    sb = seq // ts
    row = pl.BlockSpec((ts, d), lambda i: (i, 0))
    vec = pl.BlockSpec((1, d), lambda i: (0, 0))
    in_specs = [row, row, _mod_spec(d, sb, row0, comp_gate), vec, vec]
    args = [x, y, mod_rows, g.reshape(1, d), b.reshape(1, d)]
    out_shape = [jax.ShapeDtypeStruct((m, d), F32)]
    out_specs = [row]
    if next_mod is not None:
        nrow0, ncs, nch = next_mod
        in_specs += [_mod_spec(d, sb, nrow0, ncs), _mod_spec(d, sb, nrow0, nch)]
        args += [mod_rows, mod_rows]
        out_shape.append(jax.ShapeDtypeStruct((m, d), BF16))
        out_specs.append(row)
    return pl.pallas_call(
        functools.partial(_res_ln_kernel, alpha=alpha, with_mod=next_mod is not None),
        out_shape=out_shape,
        grid=(m // ts,),
        in_specs=in_specs,
        out_specs=out_specs,
        compiler_params=_params(("arbitrary",), 16 * ts * d * 4 + 8 * 2**20),
        name="res_ln",
    )(*args)


def _matmul_kernel(x_ref, w_ref, o_ref):
    w = w_ref[...].astype(BF16)
    o_ref[...] = jnp.dot(x_ref[...], w, preferred_element_type=F32).astype(o_ref.dtype)


def _matmul_tiles(m, k, n, w_bytes):
    tm = 1024 if k * 1024 * 2 <= 12 * 2**20 else 512
    tn = 512 if k * 512 * w_bytes <= 8 * 2**20 else V7X_MXU_DIM
    return min(tm, m), min(tn, n)


def _matmul(x, w, layer, out_dtype):
    m, k = x.shape
    n = w.shape[2]
    w_bytes = jnp.dtype(w.dtype).itemsize
    tm, tn = _matmul_tiles(m, k, n, w_bytes)
    out_bytes = jnp.dtype(out_dtype).itemsize
    vmem = (2 * (tm * k * 2 + k * tn * w_bytes + tm * tn * out_bytes)
            + k * tn * 2 + tm * tn * 4 + 8 * 2**20)
    return pl.pallas_call(
        _matmul_kernel,
        out_shape=jax.ShapeDtypeStruct((m, n), out_dtype),
        grid=(m // tm, n // tn),
        in_specs=[pl.BlockSpec((tm, k), lambda i, j: (i, 0)),
                  pl.BlockSpec((None, k, tn), lambda i, j: (layer, 0, j))],
        out_specs=pl.BlockSpec((tm, tn), lambda i, j: (i, j)),
        compiler_params=_params(("arbitrary", "arbitrary"), vmem),
        name="matmul",
    )(x, w)


def _lane_tile(x, width):
    return jnp.concatenate([x] * (width // V7X_LANES), axis=1)


def _lane_fold(p):
    out = p[:, :V7X_LANES]
    for c in range(1, p.shape[1] // V7X_LANES):
        out = out + p[:, c * V7X_LANES:(c + 1) * V7X_LANES]
    return out


def _dot_nt(a, b):
    return lax.dot_general(a, b, (((1,), (1,)), ((), ())), preferred_element_type=F32)


def _diff_attn_kernel(lam_ref, q_ref, k_ref, v_ref, bias_ref, g_ref, o_ref,
                      acc_ref, m_ref, l_ref, *, tile, group, n_bias, n_heads, lam_init):
    i = pl.program_id(2)
    scale = HEAD_DIM ** -0.5 * LOG2E
    width = group * tile
    hw = 2 * HEAD_DIM
    m_ref[...] = jnp.full(m_ref.shape, NEG_INF, F32)
    l_ref[...] = jnp.zeros(l_ref.shape, F32)
    acc_ref[...] = jnp.zeros(acc_ref.shape, F32)

    def body(gi, carry):
        j0 = gi * group
        rows = pl.ds(pl.multiple_of(j0 * tile, width), width)
        idx = []
        for t in range(group):
            d = i - (j0 + t)
            idx.append(jnp.where(d >= 0, jnp.minimum(d, n_bias - 2), n_bias - 1))
        chains = [(h, mi) for h in range(n_heads) for mi in range(2)]
        qk = []
        for h, mi in chains:
            cols = slice(h * hw + mi * HEAD_DIM, h * hw + (mi + 1) * HEAD_DIM)
            qk.append(_dot_nt(q_ref[:, cols], k_ref[rows, cols]))
        for c, (h, mi) in enumerate(chains):
            bt = jnp.concatenate([bias_ref[h, ix] for ix in idx], axis=1)
            vg = v_ref[rows, h * hw:(h + 1) * hw]
            s = qk[c] * scale + bt
            m_old = m_ref[c]
            m_new = jnp.maximum(m_old, jnp.max(s, axis=-1, keepdims=True))
            p = jnp.exp2(s - _lane_tile(m_new, width))
            alpha = jnp.exp2(m_old - m_new)
            l_ref[c] = alpha * l_ref[c] + _lane_fold(p)
            acc_ref[c] = (_lane_tile(alpha, hw) * acc_ref[c]
                          + jnp.dot(p.astype(vg.dtype), vg, preferred_element_type=F32))
            m_ref[c] = m_new
        return carry

    lax.fori_loop(0, i // group + 1, body, 0)

    lv = lam_ref[...]
    lam = (jnp.exp(jnp.sum(lv[0:1] * lv[1:2], axis=-1, keepdims=True))
           - jnp.exp(jnp.sum(lv[2:3] * lv[3:4], axis=-1, keepdims=True)) + lam_init)
    for h in range(n_heads):
        l0 = jnp.sum(l_ref[2 * h], axis=-1, keepdims=True)
        l1 = jnp.sum(l_ref[2 * h + 1], axis=-1, keepdims=True)
        o = acc_ref[2 * h] * (1.0 / l0) - lam * (acc_ref[2 * h + 1] * (1.0 / l1))
        y = o * lax.rsqrt(jnp.mean(o * o, axis=-1, keepdims=True) + LN_EPS)
        o_ref[:, h * hw:(h + 1) * hw] = ((y * g_ref[...]) * (1.0 - lam_init)).astype(o_ref.dtype)


def _diff_attn(qkv, lam_vecs, bias, g, *, batch, seq, n_heads, d_model, lam_init):
    m = batch * seq
    tile = DIFF_TILE
    group = DIFF_GROUP
    hp = math.gcd(DIFF_HEADS_PER_STEP, n_heads)
    nq = seq // tile
    hw = 2 * HEAD_DIM
    bw = hp * hw
    k_col0 = n_heads // hp
    v_col0 = 2 * n_heads // hp
    n_bias = bias.shape[1]
    vmem = (2 * (2 * seq * bw * 2 + hp * n_bias * tile * tile * 4 + 2 * tile * bw * 2)
            + 8 * hp * tile * group * tile * 4 + 4 * tile * bw * 4 + 8 * 2**20)
    return pl.pallas_call(
        functools.partial(_diff_attn_kernel, tile=tile, group=group, n_bias=n_bias, n_heads=hp,
                          lam_init=lam_init),
        out_shape=jax.ShapeDtypeStruct((m, d_model), BF16),
        grid=(n_heads // hp, batch, nq),
        in_specs=[
            pl.BlockSpec((4, HEAD_DIM), lambda h, b, i: (0, 0)),
            pl.BlockSpec((tile, bw), lambda h, b, i: (b * nq + i, h)),
            pl.BlockSpec((seq, bw), lambda h, b, i: (b, k_col0 + h)),
            pl.BlockSpec((seq, bw), lambda h, b, i: (b, v_col0 + h)),
            pl.BlockSpec((hp, n_bias, tile, tile), lambda h, b, i: (h, 0, 0, 0)),
            pl.BlockSpec((1, hw), lambda h, b, i: (0, 0)),
        ],
        out_specs=pl.BlockSpec((tile, bw), lambda h, b, i: (b * nq + i, h)),
        scratch_shapes=[pltpu.VMEM((2 * hp, tile, hw), F32),
                        pltpu.VMEM((2 * hp, tile, V7X_LANES), F32),
                        pltpu.VMEM((2 * hp, tile, V7X_LANES), F32)],
        compiler_params=_params(("arbitrary", "arbitrary", "arbitrary"), vmem),
        name="diff_attn",
    )(lam_vecs, qkv, qkv, qkv, bias, g.reshape(1, hw))


def _dil_attn_kernel(q_ref, k_ref, v_ref, bias_ref, g_ref, y_in_ref, o_ref, *, tile, n_win, n_heads):
    del y_in_ref
    i = pl.program_id(2)
    scale = HEAD_DIM ** -0.5 * LOG2E
    kt0 = jnp.maximum(i - (n_win - 1), 0)
    rows = pl.ds(pl.multiple_of(kt0 * tile, tile), n_win * tile)
    idx = []
    for t in range(n_win):
        d = i - (kt0 + t)
        idx.append(jnp.where(d >= 0, d, n_win))
    for h in range(n_heads):
        cols = slice(h * HEAD_DIM, (h + 1) * HEAD_DIM)
        bias = jnp.concatenate([bias_ref[h, ix] for ix in idx], axis=1)
        s = _dot_nt(q_ref[:, cols], k_ref[rows, cols]) * scale + bias
        m = jnp.max(s, axis=-1, keepdims=True)
        p = jnp.exp2(s - m)
        l = jnp.sum(p, axis=-1, keepdims=True)
        vw = v_ref[rows, cols]
        o = jnp.dot(p.astype(vw.dtype), vw, preferred_element_type=F32) * (1.0 / l)
        y = o * lax.rsqrt(jnp.mean(o * o, axis=-1, keepdims=True) + LN_EPS)
        o_ref[:, cols] = (y * g_ref[...]).astype(o_ref.dtype)


def _dil_attn(qkv, y, bias, g, *, batch, seq, n_heads, q_col0, out_col0):
    m, d_model = y.shape
    tile = DIL_TILE
    hp = math.gcd(DIL_HEADS_PER_STEP, n_heads, q_col0, out_col0)
    hw = hp * HEAD_DIM
    nq = seq // tile
    n_bias = bias.shape[1]
    n_win = n_bias - 1
    assert seq >= n_win * tile and n_heads % hp == 0 and q_col0 % hp == 0 and out_col0 % hp == 0
    qc, kc, vc, oc = (c // hp for c in (q_col0, q_col0 + n_heads, q_col0 + 2 * n_heads, out_col0))
    vmem = (2 * (2 * seq * hw * 2 + hp * n_bias * tile * tile * 4 + 2 * tile * hw * 2)
            + 4 * hp * tile * n_win * tile * 4 + 8 * 2**20)
    return pl.pallas_call(
        functools.partial(_dil_attn_kernel, tile=tile, n_win=n_win, n_heads=hp),
        out_shape=jax.ShapeDtypeStruct((m, d_model), y.dtype),
        grid=(n_heads // hp, batch, nq),
        in_specs=[
            pl.BlockSpec((tile, hw), lambda h, b, i: (b * nq + i, qc + h)),
            pl.BlockSpec((seq, hw), lambda h, b, i: (b, kc + h)),
            pl.BlockSpec((seq, hw), lambda h, b, i: (b, vc + h)),
            pl.BlockSpec((hp, n_bias, tile, tile), lambda h, b, i: (h, 0, 0, 0)),
            pl.BlockSpec((1, HEAD_DIM), lambda h, b, i: (0, 0)),
            pl.BlockSpec(memory_space=pl.ANY),
        ],
        out_specs=pl.BlockSpec((tile, hw), lambda h, b, i: (b * nq + i, oc + h)),
        input_output_aliases={5: 0},
        compiler_params=_params(("arbitrary", "arbitrary", "arbitrary"), vmem),
        name="dil_attn",
    )(qkv, qkv, qkv, bias, g.reshape(1, HEAD_DIM), y)


def _up_glu_kernel(x_ref, wg_ref, wv_ref, cw_ref, cb_ref, o_ref, hist_ref, *, tiles_per_seq,
                   row_chunks):
    i = pl.program_id(0)
    j = pl.program_id(1)
    cw = cw_ref[...]
    cb = cb_ref[...]
    rc = x_ref.shape[0] // row_chunks

    @pl.when(i % tiles_per_seq == 0)
    def _():
        hist_ref[j] = jnp.zeros(hist_ref.shape[1:], F32)

    prev = hist_ref[j]
    row = lax.broadcasted_iota(jnp.int32, (rc, wg_ref.shape[1]), 0)
    for c in range(row_chunks):
        x = x_ref[c * rc:(c + 1) * rc, :]
        g = jnp.dot(x, wg_ref[...].astype(BF16), preferred_element_type=F32)
        v = jnp.dot(x, wv_ref[...].astype(BF16), preferred_element_type=F32)
        p1 = prev[V7X_SUBLANES - 1:V7X_SUBLANES, :]
        p2 = prev[V7X_SUBLANES - 2:V7X_SUBLANES - 1, :]
        g1 = jnp.where(row == 0, p1, pltpu.roll(g, 1, axis=0))
        g2 = jnp.where(row == 0, p2, jnp.where(row == 1, p1, pltpu.roll(g, 2, axis=0)))
        gc = cw[2:3] * g + cw[1:2] * g1 + cw[0:1] * g2 + cb
        o_ref[c * rc:(c + 1) * rc, :] = (gc * jax.nn.sigmoid(gc) * v).astype(o_ref.dtype)
        prev = g[rc - V7X_SUBLANES:, :]
    hist_ref[j] = prev


def _up_glu(u, w_up, layer, conv_w, conv_b, *, seq):
    m, k = u.shape
    f = w_up.shape[2] // 2
    w_bytes = jnp.dtype(w_up.dtype).itemsize
    tc = GLU_TILE
    nj = f // tc
    tm = min(1024, seq)
    vmem = (2 * (tm * k * 2 + 2 * k * tc * w_bytes + tm * tc * 2) + 2 * k * tc * 2
            + 8 * tm * tc * 4 + 8 * 2**20)
    return pl.pallas_call(
        functools.partial(_up_glu_kernel, tiles_per_seq=seq // tm, row_chunks=GLU_ROW_CHUNKS),
        out_shape=jax.ShapeDtypeStruct((m, f), BF16),
        grid=(m // tm, nj),
        in_specs=[
            pl.BlockSpec((tm, k), lambda i, j: (i, 0)),
            pl.BlockSpec((None, k, tc), lambda i, j: (layer, 0, j)),
            pl.BlockSpec((None, k, tc), lambda i, j: (layer, 0, nj + j)),
            pl.BlockSpec((CONV_WIDTH, tc), lambda i, j: (0, j)),
            pl.BlockSpec((1, tc), lambda i, j: (0, j)),
        ],
        out_specs=pl.BlockSpec((tm, tc), lambda i, j: (i, j)),
        scratch_shapes=[pltpu.VMEM((nj, V7X_SUBLANES, tc), F32)],
        compiler_params=_params(("arbitrary", "arbitrary"), vmem),
        name="up_glu",
    )(u, w_up, w_up, conv_w, conv_b.reshape(1, f))


def kernel(x, c, w_ada, b_ada, w_in, lambda_q1, lambda_k1, lambda_q2, lambda_k2, g_diff, g_dil,
           w_o, ln1_g, ln1_b, w_up, conv_w, conv_b, w_down, ln2_g, ln2_b, rel_bias):
    batch, seq, d_model = x.shape
    depth = w_ada.shape[0]
    m = batch * seq
    diff_width = d_model // 2
    n_diff = diff_width // (2 * HEAD_DIM)
    n_dil = (d_model - diff_width) // HEAD_DIM
    assert rel_bias.shape == (NUM_BUCKETS, n_diff + n_dil)
    assert seq % (DIFF_TILE * DIFF_GROUP) == 0 and seq % ROW_TILE == 0
    assert (w_up.shape[2] // 2) % GLU_TILE == 0
    alpha = (2 * depth) ** 0.25
    SH_A, SC_A, GT_A, SH_F, SC_F, GT_F = range(N_MOD)

    mod = _adaln(c, w_ada, b_ada)
    mod_rows = mod.reshape(depth * batch * N_MOD, 1, d_model)

    n_diff_tiles = -(-(BUCKET_STARTS[-1] + DIFF_TILE - 1) // DIFF_TILE) + 1
    n_dil_tiles = max(w for w, _ in DIL_PATTERNS) // DIL_TILE + 1
    assert (n_diff_tiles - 1) * DIFF_TILE - (DIFF_TILE - 1) >= BUCKET_STARTS[-1]
    diff_bias = _bias_tiles(rel_bias, n_heads=n_diff, head_offset=0, tile=DIFF_TILE,
                            n_tiles=n_diff_tiles, dilated=False)
    dil_bias = _bias_tiles(rel_bias, n_heads=n_dil, head_offset=n_diff, tile=DIL_TILE,
                           n_tiles=n_dil_tiles, dilated=True)
    w_down_bf16 = w_down.astype(BF16)

    xf = x.reshape(m, d_model)
    u = _ln_mod(xf, mod_rows, seq=seq, row0=0, comp_scale=SC_A, comp_shift=SH_A)
    for l in range(depth):
        lam_init = 0.8 - 0.6 * math.exp(-0.3 * l)
        row0 = l * batch
        lam_vecs = jnp.stack([lambda_q1[l], lambda_k1[l], lambda_q2[l], lambda_k2[l]])
        qkv = _matmul(u, w_in, l, BF16)
        y = _diff_attn(qkv, lam_vecs, diff_bias, g_diff[l], batch=batch, seq=seq,
                       n_heads=n_diff, d_model=d_model, lam_init=lam_init)
        y = _dil_attn(qkv, y, dil_bias, g_dil[l], batch=batch, seq=seq, n_heads=n_dil,
                      q_col0=3 * diff_width // HEAD_DIM, out_col0=diff_width // HEAD_DIM)
        y = _matmul(y, w_o, l, BF16)
        xf, u = _res_ln(xf, y, mod_rows, ln1_g[l], ln1_b[l], seq=seq, alpha=alpha, row0=row0,
                        comp_gate=GT_A, next_mod=(row0, SC_F, SH_F))
        h = _up_glu(u, w_up, l, conv_w[l], conv_b[l], seq=seq)
        y = _matmul(h, w_down_bf16, l, BF16)
        if l + 1 < depth:
            xf, u = _res_ln(xf, y, mod_rows, ln2_g[l], ln2_b[l], seq=seq, alpha=alpha, row0=row0,
                            comp_gate=GT_F, next_mod=(row0 + batch, SC_A, SH_A))
        else:
            (xf,) = _res_ln(xf, y, mod_rows, ln2_g[l], ln2_b[l], seq=seq, alpha=alpha, row0=row0,
                            comp_gate=GT_F, next_mod=None)
    return xf.reshape(batch, seq, d_model)
```
